```python
import math
import jax, jax.numpy as jnp
from jax import lax
import numpy as np

D_MODEL = 1024
BATCH = 8
SEQ = 2048
DEPTH = 1

D_MIX = D_MODEL
FOX_HEADS = 8
FOX_HEAD_DIM = 64
FOX_WIDTH = FOX_HEADS * FOX_HEAD_DIM
MLSTM_HEADS = 4
MLSTM_HEAD_DIM = 128
MLSTM_WIDTH = MLSTM_HEADS * MLSTM_HEAD_DIM
MLSTM_CONV = 4
MLSTM_CHUNK = 64
Q_BLOCK = 128
N_GATE = FOX_HEADS + 2 * MLSTM_HEADS
IN_COLS = 3 * FOX_WIDTH + FOX_HEADS + 3 * MLSTM_WIDTH + 2 * MLSTM_HEADS + MLSTM_WIDTH
D_FF = 2816
FFN_CONV = 3
N_MOD = 6
EPS = 1e-6

kernel_name = "hybrid_fox_mlstm_convffn_layer"


def rmsnorm(x, g):
    xf = x.astype(jnp.float32)
    y = xf * lax.rsqrt(jnp.mean(xf * xf, axis=-1, keepdims=True) + EPS)
    return (y * g.astype(jnp.float32)).astype(x.dtype)


def causal_dwconv(x, w, b):
    K, C = w.shape
    y = lax.conv_general_dilated(x, w.reshape(K, 1, C).astype(x.dtype), window_strides=(1,),
                                 padding=[(K - 1, 0)], dimension_numbers=('NWC', 'WIO', 'NWC'),
                                 feature_group_count=C)
    return y + b.astype(x.dtype)


def split_heads(a, H):
    B, S, _ = a.shape
    return a.reshape(B, S, H, -1).transpose(0, 2, 1, 3)


def merge_heads(a):
    B, H, S, d = a.shape
    return a.transpose(0, 2, 1, 3).reshape(B, S, H * d)


def fox_attention(q, k, v, log_f):
    S = q.shape[2]
    dh = q.shape[3]
    F = jnp.cumsum(log_f.astype(jnp.float32), axis=-1)
    scale = dh ** -0.5
    outs = []
    for i in range(S // Q_BLOCK):
        q0, q1 = i * Q_BLOCK, (i + 1) * Q_BLOCK
        s = jnp.einsum('bhqd,bhkd->bhqk', q[:, :, q0:q1], k[:, :, :q1]).astype(jnp.float32) * scale
        s = s + F[:, :, q0:q1, None] - F[:, :, None, :q1]
        mask = (q0 + jnp.arange(Q_BLOCK))[:, None] >= jnp.arange(q1)[None, :]
        s = jnp.where(mask, s, -jnp.inf)
        p = jax.nn.softmax(s, axis=-1)
        outs.append(jnp.einsum('bhqk,bhkd->bhqd', p.astype(v.dtype), v[:, :, :q1]))
    return jnp.concatenate(outs, axis=2)


def mlstm_chunkwise(q, k, v, i_pre, f_pre):
    dtype = q.dtype
    B, H, S, d = q.shape
    L = MLSTM_CHUNK
    NC = S // L
    qf = q.astype(jnp.float32)
    kf = k.astype(jnp.float32) * (d ** -0.5)
    vf = v.astype(jnp.float32)
    ig = i_pre.astype(jnp.float32)
    lf = jax.nn.log_sigmoid(f_pre.astype(jnp.float32))

    def chunks(a):
        return jnp.moveaxis(a.reshape(B, H, NC, L, *a.shape[3:]), 2, 0)

    causal = jnp.tril(jnp.ones((L, L), dtype=bool))

    def step(carry, inp):
        C, n, m = carry
        qb, kb, vb, ib, lfb = inp
        b = jnp.cumsum(lfb, axis=-1)
        Dm = jnp.where(causal, b[..., :, None] - b[..., None, :] + ib[..., None, :], -jnp.inf)
        inter = b + m[..., None]
        m_row = jnp.maximum(inter, jnp.max(Dm, axis=-1))
        w_intra = jnp.exp(Dm - m_row[..., None])
        w_inter = jnp.exp(inter - m_row)
        qk = jnp.einsum('bhld,bhsd->bhls', qb, kb) * w_intra
        num = jnp.einsum('bhls,bhsd->bhld', qk, vb) + w_inter[..., None] * jnp.einsum('bhvd,bhld->bhlv', C, qb)
        den = jnp.sum(qk, axis=-1) + w_inter * jnp.einsum('bhd,bhld->bhl', n, qb)
        h = num / jnp.maximum(jnp.abs(den), jnp.exp(-m_row))[..., None]
        bL = b[..., -1]
        g = bL[..., None] - b + ib
        m_new = jnp.maximum(bL + m, jnp.max(g, axis=-1))
        wg = jnp.exp(g - m_new[..., None])
        decay = jnp.exp(bL + m - m_new)
        C_new = decay[..., None, None] * C + jnp.einsum('bhs,bhsv,bhsd->bhvd', wg, vb, kb)
        n_new = decay[..., None] * n + jnp.einsum('bhs,bhsd->bhd', wg, kb)
        return (C_new, n_new, m_new), h

    init = (jnp.zeros((B, H, d, d), jnp.float32), jnp.zeros((B, H, d), jnp.float32),
            jnp.zeros((B, H), jnp.float32))
    _, hs = lax.scan(step, init, (chunks(qf), chunks(kf), chunks(vf), chunks(ig), chunks(lf)))
    return jnp.moveaxis(hs, 0, 2).reshape(B, H, S, d).astype(dtype)


def setup_inputs(seed: int = 0) -> dict:
    key = jax.random.key(seed)
    ks = jax.random.split(key, 24)
    f32 = jnp.float32
    nrm = lambda k, shape: jax.random.normal(k, shape, f32)
    x = nrm(ks[0], (BATCH, SEQ, D_MODEL))
    c = nrm(ks[1], (BATCH, D_MODEL))
    w_ada = nrm(ks[2], (DEPTH, D_MODEL, N_MOD * D_MODEL)) * (0.5 * D_MODEL ** -0.5)
    b_ada = 0.02 * nrm(ks[3], (DEPTH, N_MOD * D_MODEL))
    g_pre_mix = 1.0 + 0.05 * nrm(ks[4], (DEPTH, D_MODEL))
    g_post_mix = 1.0 + 0.05 * nrm(ks[5], (DEPTH, D_MODEL))
    w_in = nrm(ks[6], (DEPTH, D_MODEL, IN_COLS)) * D_MODEL ** -0.5
    b_gate = jnp.concatenate([
        3.0 + 0.5 * nrm(ks[7], (DEPTH, FOX_HEADS)),
        0.1 * nrm(ks[8], (DEPTH, MLSTM_HEADS)),
        3.0 + 0.5 * nrm(ks[9], (DEPTH, MLSTM_HEADS))],
        axis=-1)
    mlstm_conv_w = nrm(ks[10], (DEPTH, MLSTM_CONV, 2 * MLSTM_WIDTH)) * MLSTM_CONV ** -0.5
    mlstm_conv_b = 0.02 * nrm(ks[11], (DEPTH, 2 * MLSTM_WIDTH))
    g_mlstm_head = 1.0 + 0.05 * nrm(ks[12], (DEPTH, MLSTM_WIDTH))
    w_out = nrm(ks[13], (DEPTH, D_MIX, D_MODEL)) * D_MIX ** -0.5
    g_pre_ffn = 1.0 + 0.05 * nrm(ks[14], (DEPTH, D_MODEL))
    g_post_ffn = 1.0 + 0.05 * nrm(ks[15], (DEPTH, D_MODEL))
    w_up = nrm(ks[16], (DEPTH, D_MODEL, 2 * D_FF)) * D_MODEL ** -0.5
    ffn_conv_w = nrm(ks[17], (DEPTH, FFN_CONV, 2 * D_FF)) * FFN_CONV ** -0.5
    ffn_conv_b = 0.02 * nrm(ks[18], (DEPTH, 2 * D_FF))
    w_down = nrm(ks[19], (DEPTH, D_FF, D_MODEL)) * D_FF ** -0.5
    return {"x": x, "c": c, "w_ada": w_ada, "b_ada": b_ada, "g_pre_mix": g_pre_mix,
            "g_post_mix": g_post_mix, "w_in": w_in, "b_gate": b_gate, "mlstm_conv_w": mlstm_conv_w,
            "mlstm_conv_b": mlstm_conv_b, "g_mlstm_head": g_mlstm_head, "w_out": w_out,
            "g_pre_ffn": g_pre_ffn, "g_post_ffn": g_post_ffn, "w_up": w_up, "ffn_conv_w": ffn_conv_w,
            "ffn_conv_b": ffn_conv_b, "w_down": w_down}


def reference(x, c, w_ada, b_ada, g_pre_mix, g_post_mix, w_in, b_gate, mlstm_conv_w, mlstm_conv_b,
              g_mlstm_head, w_out, g_pre_ffn, g_post_ffn, w_up, ffn_conv_w, ffn_conv_b, w_down):
    B, S, _ = x.shape
    sizes = (FOX_WIDTH, FOX_WIDTH, FOX_WIDTH, FOX_HEADS, MLSTM_WIDTH, MLSTM_WIDTH, MLSTM_WIDTH,
             MLSTM_HEADS, MLSTM_HEADS, MLSTM_WIDTH)
    cuts = [sum(sizes[:i + 1]) for i in range(len(sizes) - 1)]
    for l in range(DEPTH):
        mod = jax.nn.silu(c) @ w_ada[l] + b_ada[l]
        sh1, sc1, gt1, sh2, sc2, gt2 = [m[:, None, :] for m in jnp.split(mod, N_MOD, axis=-1)]

        h = rmsnorm(x, g_pre_mix[l]) * (1.0 + sc1) + sh1
        z = h @ w_in[l]
        fq, fk, fv, ff, mq, mk, mv, mi, mf, mo = jnp.split(z, cuts, axis=-1)
        bg = b_gate[l]
        fox_logf = jax.nn.log_sigmoid((ff + bg[:FOX_HEADS]).astype(jnp.float32))
        fox_o = fox_attention(split_heads(fq, FOX_HEADS), split_heads(fk, FOX_HEADS),
                              split_heads(fv, FOX_HEADS), fox_logf.transpose(0, 2, 1))
        fox_o = merge_heads(fox_o)
        mqk = jax.nn.silu(causal_dwconv(jnp.concatenate([mq, mk], axis=-1), mlstm_conv_w[l], mlstm_conv_b[l]))
        mq_c, mk_c = jnp.split(mqk, 2, axis=-1)
        i_pre = (mi + bg[FOX_HEADS:FOX_HEADS + MLSTM_HEADS]).transpose(0, 2, 1)
        f_pre = (mf + bg[FOX_HEADS + MLSTM_HEADS:]).transpose(0, 2, 1)
        m_h = mlstm_chunkwise(split_heads(mq_c, MLSTM_HEADS), split_heads(mk_c, MLSTM_HEADS),
                              split_heads(mv, MLSTM_HEADS), i_pre, f_pre)
        m_hf = m_h.astype(jnp.float32)
        m_hf = m_hf * lax.rsqrt(jnp.mean(m_hf * m_hf, axis=-1, keepdims=True) + EPS)
        mlstm_o = (merge_heads(m_hf) * g_mlstm_head[l].astype(jnp.float32)).astype(x.dtype)
        mlstm_o = jax.nn.sigmoid(mo) * mlstm_o
        mix = jnp.concatenate([fox_o, mlstm_o], axis=-1) @ w_out[l]
        x = x + gt1 * rmsnorm(mix, g_post_mix[l])

        h = rmsnorm(x, g_pre_ffn[l]) * (1.0 + sc2) + sh2
        u = causal_dwconv(h @ w_up[l], ffn_conv_w[l], ffn_conv_b[l])
        ua, ub = jnp.split(u, 2, axis=-1)
        y = (jax.nn.silu(ua) * ub) @ w_down[l]
        x = x + gt2 * rmsnorm(y, g_post_ffn[l])
    return x
```

```python
import functools

import jax
import jax.numpy as jnp
from jax import lax
from jax.experimental import pallas as pl
from jax.experimental.pallas import tpu as pltpu

F32 = jnp.float32
BF16 = jnp.bfloat16

D_MODEL = 1024
FOX_HEADS = 8
FOX_HEAD_DIM = 64
FOX_WIDTH = FOX_HEADS * FOX_HEAD_DIM
MLSTM_HEADS = 4
MLSTM_HEAD_DIM = 128
MLSTM_WIDTH = MLSTM_HEADS * MLSTM_HEAD_DIM
MLSTM_CONV = 4
D_FF = 2816
FFN_CONV = 3
N_MOD = 6
EPS = 1e-6

LANES = 128
HALO = 8
Z_COLS = 3 * FOX_WIDTH + 4 * MLSTM_WIDTH
GATE_COLS = LANES
VMEM_LIMIT = 52 * 1024 * 1024

TM_PROJ = 512
TQ = 256
MCHUNK = 128
FF_CHUNK = 256

LANE_MU, LANE_WINTER, LANE_FLOOR, LANE_WG = 8, 12, 16, 20


def _const_spec(shape):
    nd = len(shape)
    return pl.BlockSpec(shape, lambda *_: (0,) * nd, pipeline_mode=pl.Buffered(1))


def _sigmoid(x):
    return 1.0 / (1.0 + jnp.exp(-x))


def _rms_scale(x):
    return x * lax.rsqrt(jnp.mean(x * x, axis=-1, keepdims=True) + EPS)


def _ada_kernel(c_ref, w_ref, b_ref, o_ref):
    c = c_ref[...]
    sc = c * _sigmoid(c)
    o_ref[...] = jnp.dot(sc, w_ref[...], preferred_element_type=F32,
                         precision=lax.Precision.HIGHEST) + b_ref[...]


def _ada(c, w_ada, b_ada):
    B, D = c.shape
    N = w_ada.shape[1]
    tn = 1024
    return pl.pallas_call(
        _ada_kernel,
        grid=(N // tn,),
        in_specs=[pl.BlockSpec((B, D), lambda j: (0, 0)),
                  pl.BlockSpec((D, tn), lambda j: (0, j)),
                  pl.BlockSpec((1, tn), lambda j: (0, j))],
        out_specs=pl.BlockSpec((B, tn), lambda j: (0, j)),
        out_shape=jax.ShapeDtypeStruct((B, N), F32),
        compiler_params=pltpu.CompilerParams(dimension_semantics=("arbitrary",),
                                             vmem_limit_bytes=VMEM_LIMIT),
        name="ada",
    )(c, w_ada, b_ada.reshape(1, N))


def _inproj_kernel(x_ref, halo_ref, mod_ref, g_ref, w_ref, cw_ref, cb_ref, z_ref, gate_ref):
    i = pl.program_id(1)
    tm = x_ref.shape[0]
    xf = jnp.concatenate([halo_ref[...], x_ref[...]], axis=0)
    shift = mod_ref[0:1, :]
    scale = mod_ref[1:2, :]
    h = (_rms_scale(xf) * g_ref[...]) * (1.0 + scale) + shift
    hb = h.astype(BF16)
    hb_t = hb[HALO:, :]

    qk0 = 3 * FOX_WIDTH
    qk1 = qk0 + 2 * MLSTM_WIDTH
    for c0 in list(range(0, qk0, 512)) + list(range(qk1, Z_COLS, 512)):
        z_ref[:, c0:c0 + 512] = jnp.dot(hb_t, w_ref[:, c0:c0 + 512],
                                        preferred_element_type=F32).astype(BF16)
    gate_ref[...] = jnp.dot(hb_t, w_ref[:, Z_COLS:Z_COLS + GATE_COLS],
                            preferred_element_type=F32)

    first = (i > 0).astype(F32)
    row = lax.broadcasted_iota(jnp.int32, (HALO + tm, 1), 0)
    hist = jnp.where(row < HALO, first, 1.0)
    for c0 in range(qk0, qk1, 512):
        u = jnp.dot(hb, w_ref[:, c0:c0 + 512], preferred_element_type=F32) * hist
        cc = c0 - qk0
        y = cb_ref[:, cc:cc + 512]
        for k in range(MLSTM_CONV):
            off = HALO - (MLSTM_CONV - 1) + k
            y = y + u[off:off + tm, :] * cw_ref[k:k + 1, cc:cc + 512]
        y = y * _sigmoid(y)
        if cc >= MLSTM_WIDTH:
            y = y * (MLSTM_HEAD_DIM ** -0.5)
        z_ref[:, c0:c0 + 512] = y.astype(BF16)


def _inproj(x, mod, g_pre, w_all, conv_w, conv_b):
    B, S, D = x.shape
    tm = TM_PROJ
    nt = S // tm
    hb = tm // HALO
    return pl.pallas_call(
        _inproj_kernel,
        grid=(B, nt),
        in_specs=[pl.BlockSpec((None, tm, D), lambda b, i: (b, i, 0)),
                  pl.BlockSpec((None, HALO, D), lambda b, i: (b, jnp.maximum(i * hb - 1, 0), 0)),
                  pl.BlockSpec((None, N_MOD, D), lambda b, i: (b, 0, 0)),
                  _const_spec((1, D)),
                  _const_spec(w_all.shape),
                  _const_spec(conv_w.shape),
                  _const_spec(conv_b.shape)],
        out_specs=[pl.BlockSpec((None, tm, Z_COLS), lambda b, i: (b, i, 0)),
                   pl.BlockSpec((None, tm, GATE_COLS), lambda b, i: (b, i, 0))],
        out_shape=[jax.ShapeDtypeStruct((B, S, Z_COLS), BF16),
                   jax.ShapeDtypeStruct((B, S, GATE_COLS), F32)],
        compiler_params=pltpu.CompilerParams(dimension_semantics=("arbitrary", "arbitrary"),
                                             vmem_limit_bytes=VMEM_LIMIT),
        name="inproj",
    )(x, x, mod, g_pre, w_all, conv_w, conv_b)


def _gates_kernel(g_ref, bias_ref, cols_ref, rows_ref, mu_scr, m_scr, mue_scr):
    S = g_ref.shape[0]
    L = MCHUNK
    x = g_ref[...] + bias_ref[...]
    lane = lax.broadcasted_iota(jnp.int32, (S, LANES), 1)
    row = lax.broadcasted_iota(jnp.int32, (S, LANES), 0)
    is_fox = lane < FOX_HEADS
    is_m = jnp.logical_and(lane >= FOX_HEADS, lane < FOX_HEADS + MLSTM_HEADS)

    lsig = jnp.minimum(x, 0.0) - jnp.log1p(jnp.exp(-jnp.abs(x)))
    lsig_m = pltpu.roll(lsig, LANES - MLSTM_HEADS, axis=1)
    lf = jnp.where(is_fox, lsig, jnp.where(is_m, lsig_m, 0.0))

    rmod = jnp.where(is_fox, row, row & (L - 1))
    cs = lf
    sh = 1
    while sh < S:
        cs = cs + jnp.where(rmod >= sh, pltpu.roll(cs, sh, axis=0), 0.0)
        sh *= 2
    alpha = x - cs
    cm = alpha
    sh = 1
    while sh < L:
        cm = jnp.maximum(cm, jnp.where(rmod >= sh, pltpu.roll(cm, sh, axis=0), -jnp.inf))
        sh *= 2

    m = jnp.zeros((1, LANES), F32)
    for c in range(S // L):
        r0 = c * L
        mu = jnp.maximum(m, cm[r0:r0 + L, :])
        mu_end = mu[L - 1:L, :]
        mu_scr[r0:r0 + L, :] = mu
        m_scr[r0:r0 + L, :] = jnp.broadcast_to(m, (L, LANES))
        mue_scr[r0:r0 + L, :] = jnp.broadcast_to(mu_end, (L, LANES))
        m = cs[r0 + L - 1:r0 + L, :] + mu_end

    mu = mu_scr[...]
    w_inter = jnp.exp(m_scr[...] - mu)
    floor = jnp.exp(-(mu + cs))
    wg = jnp.exp(alpha - mue_scr[...])

    def put(v, lane0):
        moved = pltpu.roll(v, lane0 - FOX_HEADS, axis=1) if lane0 != FOX_HEADS else v
        return jnp.where(jnp.logical_and(lane >= lane0, lane < lane0 + MLSTM_HEADS), moved, 0.0)

    cols = jnp.where(is_fox, cs, 0.0) + put(mu, LANE_MU) + put(w_inter, LANE_WINTER) \
        + put(floor, LANE_FLOOR) + put(wg, LANE_WG)
    cols_ref[...] = cols
    rows = jnp.where(is_fox, cs, jnp.where(is_m, alpha, 0.0)).T
    rows_ref[...] = rows[0:16, :]


def _gates(gates, bias):
    B, S, _ = gates.shape
    return pl.pallas_call(
        _gates_kernel,
        grid=(B,),
        in_specs=[pl.BlockSpec((None, S, LANES), lambda b: (b, 0, 0)),
                  _const_spec((1, LANES))],
        out_specs=[pl.BlockSpec((None, S, LANES), lambda b: (b, 0, 0)),
                   pl.BlockSpec((None, 16, S), lambda b: (b, 0, 0))],
        out_shape=[jax.ShapeDtypeStruct((B, S, LANES), F32),
                   jax.ShapeDtypeStruct((B, 16, S), F32)],
        scratch_shapes=[pltpu.VMEM((S, LANES), F32)] * 3,
        compiler_params=pltpu.CompilerParams(dimension_semantics=("arbitrary",),
                                             vmem_limit_bytes=VMEM_LIMIT),
        name="gates",
    )(gates, bias)


def _fox_kernel(q_ref, k_ref, v_ref, cols_ref, rows_ref, o_ref):
    hp = pl.program_id(1)
    qi = pl.program_id(2)
    tq = q_ref.shape[0]
    q2 = q_ref[...]
    lane = lax.broadcasted_iota(jnp.int32, (tq, LANES), 1)
    colsq = cols_ref[...]
    ones = jnp.ones((tq, LANES), BF16)
    nt = (((1,), (1,)), ((), ()))

    qm, fq = [], []
    for hh in range(2):
        in_head = jnp.logical_and(lane >= hh * FOX_HEAD_DIM, lane < (hh + 1) * FOX_HEAD_DIM)
        qm.append(jnp.where(in_head, q2, jnp.zeros_like(q2)))
        fq.append(jnp.sum(jnp.where(lane == 2 * hp + hh, colsq, 0.0), axis=-1, keepdims=True))

    def block(j, masked, carry):
        k0 = pl.multiple_of(j * tq, tq)
        kb = k_ref[pl.ds(k0, tq), :]
        va = jnp.concatenate([v_ref[pl.ds(k0, tq), :], ones], axis=1)
        out = []
        for hh in range(2):
            s = lax.dot_general(qm[hh], kb, nt, preferred_element_type=F32)
            s = s + fq[hh] - rows_ref[hh, pl.ds(j, 1), :]
            if masked:
                r = lax.broadcasted_iota(jnp.int32, (tq, tq), 0)
                cidx = lax.broadcasted_iota(jnp.int32, (tq, tq), 1)
                s = jnp.where(r >= cidx, s, -jnp.inf)
            bm = jnp.max(s, axis=-1, keepdims=True)
            if carry is None:
                m_new = bm
                p = jnp.exp(s - m_new)
                acc = jnp.dot(p.astype(BF16), va, preferred_element_type=F32)
            else:
                m_old, acc_old = carry[2 * hh], carry[2 * hh + 1]
                m_new = jnp.maximum(m_old, bm)
                p = jnp.exp(s - m_new)
                acc = jnp.exp(m_old - m_new) * acc_old + jnp.dot(p.astype(BF16), va,
                                                                 preferred_element_type=F32)
            out += [m_new, acc]
        return tuple(out)

    carry = block(qi, True, None)
    carry = lax.fori_loop(0, qi, lambda j, cr: block(j, False, cr), carry)

    o = []
    for hh in range(2):
        acc = carry[2 * hh + 1]
        o.append(acc[:, :LANES] / acc[:, LANES:])
    o_ref[...] = jnp.where(lane < FOX_HEAD_DIM, o[0], o[1]).astype(BF16)


def _fox(z, cols, rows4):
    B, S, _ = z.shape
    nq = S // TQ
    npair = FOX_HEADS // 2
    kv_off = FOX_WIDTH // LANES
    return pl.pallas_call(
        _fox_kernel,
        grid=(B, npair, nq),
        in_specs=[pl.BlockSpec((None, TQ, LANES), lambda b, h, i: (b, i, h)),
                  pl.BlockSpec((None, S, LANES), lambda b, h, i: (b, 0, kv_off + h)),
                  pl.BlockSpec((None, S, LANES), lambda b, h, i: (b, 0, 2 * kv_off + h)),
                  pl.BlockSpec((None, TQ, LANES), lambda b, h, i: (b, i, 0)),
                  pl.BlockSpec((None, 2, nq, TQ), lambda b, h, i: (b, h, 0, 0))],
        out_specs=pl.BlockSpec((None, TQ, LANES), lambda b, h, i: (b, i, h)),
        out_shape=jax.ShapeDtypeStruct((B, S, FOX_WIDTH), BF16),
        compiler_params=pltpu.CompilerParams(
            dimension_semantics=("arbitrary", "arbitrary", "arbitrary"),
            vmem_limit_bytes=VMEM_LIMIT),
        name="fox",
    )(z, z, z, cols, rows4)


def _mlstm_kernel(q_ref, k_ref, v_ref, og_ref, cols_ref, rows_ref, gh_ref, o_ref):
    S = q_ref.shape[0]
    L = MCHUNK
    d = MLSTM_HEAD_DIM
    r = lax.broadcasted_iota(jnp.int32, (L, L), 0)
    cidx = lax.broadcasted_iota(jnp.int32, (L, L), 1)
    causal = r >= cidx
    ones = jnp.ones((L, d), BF16)
    nt = (((1,), (1,)), ((), ()))
    tn = (((0,), (0,)), ((), ()))

    def chunk(c, states):
        r0 = pl.multiple_of(c * L, L)
        colsc = cols_ref[pl.ds(r0, L), :]
        new_states = []
        for h in range(MLSTM_HEADS):
            hs = slice(h * d, (h + 1) * d)
            q = q_ref[pl.ds(r0, L), hs]
            k = k_ref[pl.ds(r0, L), hs]
            v = v_ref[pl.ds(r0, L), hs]
            mu = colsc[:, LANE_MU + h:LANE_MU + h + 1]
            w_inter = colsc[:, LANE_WINTER + h:LANE_WINTER + h + 1]
            floor = colsc[:, LANE_FLOOR + h:LANE_FLOOR + h + 1]
            wg = colsc[:, LANE_WG + h:LANE_WG + h + 1]
            alpha = rows_ref[h, pl.ds(c, 1), :]
            w = jnp.where(causal, jnp.exp(alpha - mu), 0.0)
            sq = lax.dot_general(q, k, nt, preferred_element_type=F32)
            va = jnp.concatenate([v, ones], axis=1)
            ct = states[h]
            tot = jnp.dot((sq * w).astype(BF16), va, preferred_element_type=F32) \
                + w_inter * jnp.dot(q, ct.astype(BF16), preferred_element_type=F32)
            num = tot[:, :d]
            den = tot[:, d:]
            hv = num / jnp.maximum(jnp.abs(den), floor)
            hn = _rms_scale(hv) * gh_ref[:, hs]
            og = _sigmoid(og_ref[pl.ds(r0, L), hs].astype(F32))
            o_ref[pl.ds(r0, L), hs] = (og * hn).astype(BF16)
            wv = (wg * va.astype(F32)).astype(BF16)
            decay = w_inter[L - 1:L, :]
            new_states.append(decay * ct + lax.dot_general(k, wv, tn, preferred_element_type=F32))
        return tuple(new_states)

    init = tuple(jnp.zeros((d, 2 * d), F32) for _ in range(MLSTM_HEADS))
    lax.fori_loop(0, S // L, chunk, init)


def _mlstm(z, cols, rows4, g_head):
    B, S, _ = z.shape
    blk = MLSTM_WIDTH
    base = 3 * FOX_WIDTH // blk
    nc = S // MCHUNK
    return pl.pallas_call(
        _mlstm_kernel,
        grid=(B,),
        in_specs=[pl.BlockSpec((None, S, blk), lambda b: (b, 0, base)),
                  pl.BlockSpec((None, S, blk), lambda b: (b, 0, base + 1)),
                  pl.BlockSpec((None, S, blk), lambda b: (b, 0, base + 2)),
                  pl.BlockSpec((None, S, blk), lambda b: (b, 0, base + 3)),
                  pl.BlockSpec((None, S, LANES), lambda b: (b, 0, 0)),
                  pl.BlockSpec((None, MLSTM_HEADS, nc, MCHUNK), lambda b: (b, 2, 0, 0)),
                  _const_spec((1, blk))],
        out_specs=pl.BlockSpec((None, S, blk), lambda b: (b, 0, 0)),
        out_shape=jax.ShapeDtypeStruct((B, S, blk), BF16),
        compiler_params=pltpu.CompilerParams(dimension_semantics=("arbitrary",),
                                             vmem_limit_bytes=VMEM_LIMIT),
        name="mlstm",
    )(z, z, z, z, cols, rows4, g_head)


def _outproj_kernel(fo_ref, mo_ref, x_ref, mod_ref, g_ref, w_ref, o_ref):
    mix = jnp.dot(fo_ref[...], w_ref[:FOX_WIDTH, :], preferred_element_type=F32) \
        + jnp.dot(mo_ref[...], w_ref[FOX_WIDTH:, :], preferred_element_type=F32)
    gate = mod_ref[2:3, :]
    o_ref[...] = x_ref[...] + gate * (_rms_scale(mix) * g_ref[...])


def _outproj(fox_o, mlstm_o, x, mod, g_post, w_out):
    B, S, D = x.shape
    tm = TM_PROJ
    return pl.pallas_call(
        _outproj_kernel,
        grid=(B, S // tm),
        in_specs=[pl.BlockSpec((None, tm, FOX_WIDTH), lambda b, i: (b, i, 0)),
                  pl.BlockSpec((None, tm, MLSTM_WIDTH), lambda b, i: (b, i, 0)),
                  pl.BlockSpec((None, tm, D), lambda b, i: (b, i, 0)),
                  pl.BlockSpec((None, N_MOD, D), lambda b, i: (b, 0, 0)),
                  _const_spec((1, D)),
                  _const_spec(w_out.shape)],
        out_specs=pl.BlockSpec((None, tm, D), lambda b, i: (b, i, 0)),
        out_shape=jax.ShapeDtypeStruct((B, S, D), F32),
        compiler_params=pltpu.CompilerParams(dimension_semantics=("arbitrary", "arbitrary"),
                                             vmem_limit_bytes=VMEM_LIMIT),
        name="outproj",
    )(fox_o, mlstm_o, x, mod, g_post, w_out)


def _ffn_kernel(x_ref, halo_ref, mod_ref, gpre_ref, gpost_ref, wup_ref, cw_ref, cb_ref, wdn_ref,
                o_ref):
    i = pl.program_id(1)
    tm = x_ref.shape[0]
    x = x_ref[...]
    xf = jnp.concatenate([halo_ref[...], x], axis=0)
    shift = mod_ref[3:4, :]
    scale = mod_ref[4:5, :]
    gate = mod_ref[5:6, :]
    hb = ((_rms_scale(xf) * gpre_ref[...]) * (1.0 + scale) + shift).astype(BF16)
    first = (i > 0).astype(F32)
    row = lax.broadcasted_iota(jnp.int32, (HALO + tm, 1), 0)
    hist = jnp.where(row < HALO, first, 1.0)

    def conv(u, c0):
        u = u * hist
        y = cb_ref[:, c0:c0 + FF_CHUNK]
        for k in range(FFN_CONV):
            off = HALO - (FFN_CONV - 1) + k
            y = y + u[off:off + tm, :] * cw_ref[k:k + 1, c0:c0 + FF_CHUNK]
        return y

    acc = jnp.zeros((tm, D_MODEL), F32)
    for c0 in range(0, D_FF, FF_CHUNK):
        ua = conv(jnp.dot(hb, wup_ref[:, c0:c0 + FF_CHUNK], preferred_element_type=F32), c0)
        ub = conv(jnp.dot(hb, wup_ref[:, D_FF + c0:D_FF + c0 + FF_CHUNK],
                          preferred_element_type=F32), D_FF + c0)
        g = (ua * _sigmoid(ua) * ub).astype(BF16)
        acc = acc + jnp.dot(g, wdn_ref[c0:c0 + FF_CHUNK, :], preferred_element_type=F32)
    o_ref[...] = x + gate * (_rms_scale(acc) * gpost_ref[...])


def _ffn(x, mod, g_pre, g_post, w_up, conv_w, conv_b, w_down):
    B, S, D = x.shape
    tm = TM_PROJ
    hb = tm // HALO
    return pl.pallas_call(
        _ffn_kernel,
        grid=(B, S // tm),
        in_specs=[pl.BlockSpec((None, tm, D), lambda b, i: (b, i, 0)),
                  pl.BlockSpec((None, HALO, D), lambda b, i: (b, jnp.maximum(i * hb - 1, 0), 0)),
                  pl.BlockSpec((None, N_MOD, D), lambda b, i: (b, 0, 0)),
                  _const_spec((1, D)),
                  _const_spec((1, D)),
                  _const_spec(w_up.shape),
                  _const_spec(conv_w.shape),
                  _const_spec(conv_b.shape),
                  _const_spec(w_down.shape)],
        out_specs=pl.BlockSpec((None, tm, D), lambda b, i: (b, i, 0)),
        out_shape=jax.ShapeDtypeStruct((B, S, D), F32),
        compiler_params=pltpu.CompilerParams(dimension_semantics=("arbitrary", "arbitrary"),
                                             vmem_limit_bytes=VMEM_LIMIT),
        name="ffn",
    )(x, x, mod, g_pre, g_post, w_up, conv_w, conv_b, w_down)


def _layer(x, c, w_ada, b_ada, g_pre_mix, g_post_mix, w_in, b_gate, mlstm_conv_w, mlstm_conv_b,
           g_mlstm_head, w_out, g_pre_ffn, g_post_ffn, w_up, ffn_conv_w, ffn_conv_b, w_down):
    B, S, D = x.shape
    mod = _ada(c, w_ada, b_ada).reshape(B, N_MOD, D)

    sizes = (FOX_WIDTH, FOX_WIDTH, FOX_WIDTH, FOX_HEADS, MLSTM_WIDTH, MLSTM_WIDTH, MLSTM_WIDTH,
             MLSTM_HEADS, MLSTM_HEADS, MLSTM_WIDTH)
    offs = [0]
    for s in sizes:
        offs.append(offs[-1] + s)
    col = lambda j: w_in[:, offs[j]:offs[j + 1]]
    n_gate = FOX_HEADS + 2 * MLSTM_HEADS
    w_all = jnp.concatenate(
        [col(0) * (FOX_HEAD_DIM ** -0.5), col(1), col(2), col(4), col(5), col(6), col(9),
         col(3), col(7), col(8), jnp.zeros((D, GATE_COLS - n_gate), F32)], axis=1).astype(BF16)
    bias = jnp.concatenate([b_gate, jnp.zeros((GATE_COLS - n_gate,), F32)]).reshape(1, GATE_COLS)

    z, gates = _inproj(x, mod, g_pre_mix.reshape(1, D), w_all, mlstm_conv_w,
                       mlstm_conv_b.reshape(1, -1))
    cols, rows = _gates(gates, bias)
    fox_o = _fox(z, cols, rows.reshape(B, 16, S // TQ, TQ))
    mlstm_o = _mlstm(z, cols, rows.reshape(B, 16, S // MCHUNK, MCHUNK),
                     g_mlstm_head.reshape(1, -1))
    x1 = _outproj(fox_o, mlstm_o, x, mod, g_post_mix.reshape(1, D), w_out.astype(BF16))
    return _ffn(x1, mod, g_pre_ffn.reshape(1, D), g_post_ffn.reshape(1, D), w_up.astype(BF16),
                ffn_conv_w, ffn_conv_b.reshape(1, -1), w_down.astype(BF16))


def kernel(x, c, w_ada, b_ada, g_pre_mix, g_post_mix, w_in, b_gate, mlstm_conv_w, mlstm_conv_b,
           g_mlstm_head, w_out, g_pre_ffn, g_post_ffn, w_up, ffn_conv_w, ffn_conv_b, w_down):
    for l in range(w_ada.shape[0]):
        x = _layer(x, c, w_ada[l], b_ada[l], g_pre_mix[l], g_post_mix[l], w_in[l], b_gate[l],
                   mlstm_conv_w[l], mlstm_conv_b[l], g_mlstm_head[l], w_out[l], g_pre_ffn[l],
                   g_post_ffn[l], w_up[l], ffn_conv_w[l], ffn_conv_b[l], w_down[l])
    return x
```

```python
import functools

import jax
import jax.numpy as jnp
import numpy as np
from jax import lax
from jax.experimental import pallas as pl
from jax.experimental.pallas import tpu as pltpu

F32 = jnp.float32
BF16 = jnp.bfloat16

D_MODEL = 1024
FOX_HEADS = 8
FOX_HEAD_DIM = 64
FOX_WIDTH = FOX_HEADS * FOX_HEAD_DIM
MLSTM_HEADS = 4
MLSTM_HEAD_DIM = 128
MLSTM_WIDTH = MLSTM_HEADS * MLSTM_HEAD_DIM
MLSTM_CONV = 4
D_FF = 2816
FFN_CONV = 3
N_MOD = 6
EPS = 1e-6

LANES = 128
HALO = 8
Z_COLS = 3 * FOX_WIDTH + 4 * MLSTM_WIDTH
GATE_COLS = LANES
VMEM_LIMIT = 52 * 1024 * 1024

TM_PROJ = 512
TQ = 256
MCHUNK = 128
FF_CHUNK = 256

LANE_MU, LANE_WINTER, LANE_FLOOR, LANE_WG = 8, 12, 16, 20
LOG2E = 1.4426950408889634
FOX_BIAS = 6
M_INIT = -1e30


def _const_spec(shape):
    nd = len(shape)
    return pl.BlockSpec(shape, lambda *_: (0,) * nd, pipeline_mode=pl.Buffered(1))


def _sigmoid(x):
    return 1.0 / (1.0 + jnp.exp(-x))


def _rms_scale(x):
    return x * lax.rsqrt(jnp.mean(x * x, axis=-1, keepdims=True) + EPS)


def _ada_kernel(c_ref, w_ref, b_ref, o_ref):
    c = c_ref[...]
    sc = c * _sigmoid(c)
    o_ref[...] = jnp.dot(sc, w_ref[...], preferred_element_type=F32,
                         precision=lax.Precision.HIGHEST) + b_ref[...]


def _ada(c, w_ada, b_ada):
    B, D = c.shape
    N = w_ada.shape[1]
    tn = 1024
    return pl.pallas_call(
        _ada_kernel,
        grid=(N // tn,),
        in_specs=[pl.BlockSpec((B, D), lambda j: (0, 0)),
                  pl.BlockSpec((D, tn), lambda j: (0, j)),
                  pl.BlockSpec((1, tn), lambda j: (0, j))],
        out_specs=pl.BlockSpec((B, tn), lambda j: (0, j)),
        out_shape=jax.ShapeDtypeStruct((B, N), F32),
        compiler_params=pltpu.CompilerParams(dimension_semantics=("arbitrary",),
                                             vmem_limit_bytes=VMEM_LIMIT),
        name="ada",
    )(c, w_ada, b_ada.reshape(1, N))


def _inproj_kernel(x_ref, halo_ref, mod_ref, g_ref, w_ref, cw_ref, cb_ref, z_ref, gate_ref):
    i = pl.program_id(1)
    tm = x_ref.shape[0]
    xf = jnp.concatenate([halo_ref[...], x_ref[...]], axis=0)
    shift = mod_ref[0:1, :]
    scale = mod_ref[1:2, :]
    h = (_rms_scale(xf) * g_ref[...]) * (1.0 + scale) + shift
    hb = h.astype(BF16)
    hb_t = hb[HALO:, :]

    qk0 = 3 * FOX_WIDTH
    qk1 = qk0 + 2 * MLSTM_WIDTH
    for c0 in list(range(0, qk0, 512)) + list(range(qk1, Z_COLS, 512)):
        z_ref[:, c0:c0 + 512] = jnp.dot(hb_t, w_ref[:, c0:c0 + 512],
                                        preferred_element_type=F32).astype(BF16)
    gate_ref[...] = jnp.dot(hb_t, w_ref[:, Z_COLS:Z_COLS + GATE_COLS],
                            preferred_element_type=F32)

    first = (i > 0).astype(F32)
    row = lax.broadcasted_iota(jnp.int32, (HALO + tm, 1), 0)
    hist = jnp.where(row < HALO, first, 1.0)
    for c0 in range(qk0, qk1, 512):
        u = jnp.dot(hb, w_ref[:, c0:c0 + 512], preferred_element_type=F32) * hist
        cc = c0 - qk0
        y = cb_ref[:, cc:cc + 512]
        for k in range(MLSTM_CONV):
            off = HALO - (MLSTM_CONV - 1) + k
            y = y + u[off:off + tm, :] * cw_ref[k:k + 1, cc:cc + 512]
        y = y * _sigmoid(y)
        if cc >= MLSTM_WIDTH:
            y = y * (MLSTM_HEAD_DIM ** -0.5)
        z_ref[:, c0:c0 + 512] = y.astype(BF16)


def _inproj(x, mod, g_pre, w_all, conv_w, conv_b):
    B, S, D = x.shape
    tm = TM_PROJ
    nt = S // tm
    hb = tm // HALO
    return pl.pallas_call(
        _inproj_kernel,
        grid=(B, nt),
        in_specs=[pl.BlockSpec((None, tm, D), lambda b, i: (b, i, 0)),
                  pl.BlockSpec((None, HALO, D), lambda b, i: (b, jnp.maximum(i * hb - 1, 0), 0)),
                  pl.BlockSpec((None, N_MOD, D), lambda b, i: (b, 0, 0)),
                  _const_spec((1, D)),
                  _const_spec(w_all.shape),
                  _const_spec(conv_w.shape),
                  _const_spec(conv_b.shape)],
        out_specs=[pl.BlockSpec((None, tm, Z_COLS), lambda b, i: (b, i, 0)),
                   pl.BlockSpec((None, tm, GATE_COLS), lambda b, i: (b, i, 0))],
        out_shape=[jax.ShapeDtypeStruct((B, S, Z_COLS), BF16),
                   jax.ShapeDtypeStruct((B, S, GATE_COLS), F32)],
        compiler_params=pltpu.CompilerParams(dimension_semantics=("arbitrary", "arbitrary"),
                                             vmem_limit_bytes=VMEM_LIMIT),
        name="inproj",
    )(x, x, mod, g_pre, w_all, conv_w, conv_b)


def _gates_kernel(g_ref, bias_ref, perm_ref, cols_ref, rows_ref, qb_ref, kb_ref,
                  mu_scr, m_scr, mue_scr):
    S = g_ref.shape[0]
    L = MCHUNK
    x = g_ref[...] + bias_ref[...]
    lane = lax.broadcasted_iota(jnp.int32, (S, LANES), 1)
    row = lax.broadcasted_iota(jnp.int32, (S, LANES), 0)
    is_fox = lane < FOX_HEADS
    is_m = jnp.logical_and(lane >= FOX_HEADS, lane < FOX_HEADS + MLSTM_HEADS)

    lsig = jnp.minimum(x, 0.0) - jnp.log1p(jnp.exp(-jnp.abs(x)))
    lsig_m = pltpu.roll(lsig, LANES - MLSTM_HEADS, axis=1)
    lf = jnp.where(is_fox, lsig, jnp.where(is_m, lsig_m, 0.0))

    rmod = jnp.where(is_fox, row, row & (L - 1))
    cs = lf
    sh = 1
    while sh < S:
        cs = cs + jnp.where(rmod >= sh, pltpu.roll(cs, sh, axis=0), 0.0)
        sh *= 2
    f2 = jnp.where(is_fox, cs * LOG2E, 0.0)
    hi = f2.astype(BF16)
    r1 = f2 - hi.astype(F32)
    mid = r1.astype(BF16)
    lo = (r1 - mid.astype(F32)).astype(BF16)
    hi = jnp.where(lane == FOX_HEADS, jnp.ones_like(hi), hi)
    parts = (hi, mid, lo)
    qb_ref[...] = sum(jnp.dot(parts[t], perm_ref[t], preferred_element_type=F32)
                      for t in range(3)).astype(BF16)
    kb_ref[...] = sum(jnp.dot(parts[t], perm_ref[3 + t], preferred_element_type=F32)
                      for t in range(3)).astype(BF16)

    alpha = x - cs
    cm = alpha
    sh = 1
    while sh < L:
        cm = jnp.maximum(cm, jnp.where(rmod >= sh, pltpu.roll(cm, sh, axis=0), -jnp.inf))
        sh *= 2

    m = jnp.zeros((1, LANES), F32)
    for c in range(S // L):
        r0 = c * L
        mu = jnp.maximum(m, cm[r0:r0 + L, :])
        mu_end = mu[L - 1:L, :]
        mu_scr[r0:r0 + L, :] = mu
        m_scr[r0:r0 + L, :] = jnp.broadcast_to(m, (L, LANES))
        mue_scr[r0:r0 + L, :] = jnp.broadcast_to(mu_end, (L, LANES))
        m = cs[r0 + L - 1:r0 + L, :] + mu_end

    mu = mu_scr[...]
    w_inter = jnp.exp(m_scr[...] - mu)
    floor = jnp.exp(-(mu + cs))
    wg = jnp.exp(alpha - mue_scr[...])

    def put(v, lane0):
        moved = pltpu.roll(v, lane0 - FOX_HEADS, axis=1) if lane0 != FOX_HEADS else v
        return jnp.where(jnp.logical_and(lane >= lane0, lane < lane0 + MLSTM_HEADS), moved, 0.0)

    cols = jnp.where(is_fox, cs, 0.0) + put(mu, LANE_MU) + put(w_inter, LANE_WINTER) \
        + put(floor, LANE_FLOOR) + put(wg, LANE_WG)
    cols_ref[...] = cols
    rows = jnp.where(is_fox, cs, jnp.where(is_m, alpha, 0.0)).T
    rows_ref[...] = rows[0:16, :]


def _fox_bias_perm():
    p = np.zeros((6, LANES, LANES), np.float32)
    for h in range(FOX_HEADS):
        for t in range(3):
            p[t, h, FOX_BIAS * h + t] = 1.0
            p[0, FOX_HEADS, FOX_BIAS * h + 3 + t] = 1.0
            p[3 + t, h, FOX_BIAS * h + 3 + t] = -1.0
            p[3, FOX_HEADS, FOX_BIAS * h + t] = 1.0
    return jnp.asarray(p, BF16)


def _gates(gates, bias):
    B, S, _ = gates.shape
    tok = pl.BlockSpec((None, S, LANES), lambda b: (b, 0, 0))
    return pl.pallas_call(
        _gates_kernel,
        grid=(B,),
        in_specs=[tok, _const_spec((1, LANES)), _const_spec((6, LANES, LANES))],
        out_specs=[tok, pl.BlockSpec((None, 16, S), lambda b: (b, 0, 0)), tok, tok],
        out_shape=[jax.ShapeDtypeStruct((B, S, LANES), F32),
                   jax.ShapeDtypeStruct((B, 16, S), F32),
                   jax.ShapeDtypeStruct((B, S, LANES), BF16),
                   jax.ShapeDtypeStruct((B, S, LANES), BF16)],
        scratch_shapes=[pltpu.VMEM((S, LANES), F32)] * 3,
        compiler_params=pltpu.CompilerParams(dimension_semantics=("arbitrary",),
                                             vmem_limit_bytes=VMEM_LIMIT),
        name="gates",
    )(gates, bias, _fox_bias_perm())


def _fox_kernel(q_ref, k_ref, v_ref, qb_ref, kb_ref, o_ref, qa_scr, acc_scr, m_scr):
    qi = pl.program_id(1)
    tq = q_ref.shape[0]
    lane = lax.broadcasted_iota(jnp.int32, (tq, LANES), 1)
    ones = jnp.ones((tq, LANES), BF16)
    nt = (((1,), (1,)), ((), ()))
    qb = qb_ref[...]
    zero = jnp.zeros((tq, LANES), BF16)

    for h in range(FOX_HEADS):
        hp, hh = divmod(h, 2)
        q2 = q_ref[:, hp * LANES:(hp + 1) * LANES]
        in_head = jnp.logical_and(lane >= hh * FOX_HEAD_DIM, lane < (hh + 1) * FOX_HEAD_DIM)
        in_bias = jnp.logical_and(lane >= FOX_BIAS * h, lane < FOX_BIAS * (h + 1))
        qa_scr[h] = jnp.concatenate([jnp.where(in_head, q2, zero), jnp.where(in_bias, qb, zero)],
                                    axis=1)
        acc_scr[h] = jnp.zeros((tq, 2 * LANES), F32)
        m_scr[h] = jnp.full((tq, LANES), M_INIT, F32)

    def block(j, masked):
        k0 = pl.multiple_of(j * tq, tq)
        kbias = kb_ref[pl.ds(k0, tq), :]
        if masked:
            r = lax.broadcasted_iota(jnp.int32, (tq, tq), 0)
            cidx = lax.broadcasted_iota(jnp.int32, (tq, tq), 1)
            mbias = jnp.where(r >= cidx, 0.0, -jnp.inf)
        for hp in range(FOX_HEADS // 2):
            cs = slice(hp * LANES, (hp + 1) * LANES)
            ka = jnp.concatenate([k_ref[pl.ds(k0, tq), cs], kbias], axis=1)
            va = jnp.concatenate([v_ref[pl.ds(k0, tq), cs], ones], axis=1)
            for h in (2 * hp, 2 * hp + 1):
                s = lax.dot_general(qa_scr[h], ka, nt, preferred_element_type=F32)
                if masked:
                    s = s + mbias
                m_old = m_scr[h]
                m_new = jnp.maximum(m_old, jnp.max(s, axis=-1, keepdims=True))
                p = jnp.exp2(s - jnp.concatenate([m_new, m_new], axis=1))
                corr = jnp.exp2(m_old - m_new)
                acc_scr[h] = acc_scr[h] * jnp.concatenate([corr, corr], axis=1) \
                    + jnp.dot(p.astype(BF16), va, preferred_element_type=F32)
                m_scr[h] = m_new

    def body(j, carry):
        block(j, False)
        return carry

    lax.fori_loop(0, qi, body, 0)
    block(qi, True)

    for hp in range(FOX_HEADS // 2):
        o = []
        for h in (2 * hp, 2 * hp + 1):
            acc = acc_scr[h]
            o.append(acc[:, :LANES] / acc[:, LANES:])
        o_ref[:, hp * LANES:(hp + 1) * LANES] = \
            jnp.where(lane < FOX_HEAD_DIM, o[0], o[1]).astype(BF16)


def _fox(z, qb, kb):
    B, S, _ = z.shape
    nq = S // TQ
    return pl.pallas_call(
        _fox_kernel,
        grid=(B, nq),
        in_specs=[pl.BlockSpec((None, TQ, FOX_WIDTH), lambda b, i: (b, i, 0)),
                  pl.BlockSpec((None, S, FOX_WIDTH), lambda b, i: (b, 0, 1)),
                  pl.BlockSpec((None, S, FOX_WIDTH), lambda b, i: (b, 0, 2)),
                  pl.BlockSpec((None, TQ, LANES), lambda b, i: (b, i, 0)),
                  pl.BlockSpec((None, S, LANES), lambda b, i: (b, 0, 0))],
        out_specs=pl.BlockSpec((None, TQ, FOX_WIDTH), lambda b, i: (b, i, 0)),
        out_shape=jax.ShapeDtypeStruct((B, S, FOX_WIDTH), BF16),
        scratch_shapes=[pltpu.VMEM((FOX_HEADS, TQ, 2 * LANES), BF16),
                        pltpu.VMEM((FOX_HEADS, TQ, 2 * LANES), F32),
                        pltpu.VMEM((FOX_HEADS, TQ, LANES), F32)],
        compiler_params=pltpu.CompilerParams(dimension_semantics=("arbitrary", "arbitrary"),
                                             vmem_limit_bytes=VMEM_LIMIT),
        name="fox",
    )(z, z, z, qb, kb)


def _mlstm_kernel(q_ref, k_ref, v_ref, og_ref, cols_ref, rows_ref, gh_ref, o_ref):
    S = q_ref.shape[0]
    L = MCHUNK
    d = MLSTM_HEAD_DIM
    r = lax.broadcasted_iota(jnp.int32, (L, L), 0)
    cidx = lax.broadcasted_iota(jnp.int32, (L, L), 1)
    causal = r >= cidx
    ones = jnp.ones((L, d), BF16)
    nt = (((1,), (1,)), ((), ()))
    tn = (((0,), (0,)), ((), ()))

    def chunk(c, states):
        r0 = pl.multiple_of(c * L, L)
        colsc = cols_ref[pl.ds(r0, L), :]
        new_states = []
        for h in range(MLSTM_HEADS):
            hs = slice(h * d, (h + 1) * d)
            q = q_ref[pl.ds(r0, L), hs]
            k = k_ref[pl.ds(r0, L), hs]
            v = v_ref[pl.ds(r0, L), hs]
            mu = colsc[:, LANE_MU + h:LANE_MU + h + 1]
            w_inter = colsc[:, LANE_WINTER + h:LANE_WINTER + h + 1]
            floor = colsc[:, LANE_FLOOR + h:LANE_FLOOR + h + 1]
            wg = colsc[:, LANE_WG + h:LANE_WG + h + 1]
            alpha = rows_ref[h, pl.ds(c, 1), :]
            w = jnp.where(causal, jnp.exp(alpha - mu), 0.0)
            sq = lax.dot_general(q, k, nt, preferred_element_type=F32)
            va = jnp.concatenate([v, ones], axis=1)
            ct = states[h]
            tot = jnp.dot((sq * w).astype(BF16), va, preferred_element_type=F32) \
                + w_inter * jnp.dot(q, ct.astype(BF16), preferred_element_type=F32)
            num = tot[:, :d]
            den = tot[:, d:]
            hv = num / jnp.maximum(jnp.abs(den), floor)
            hn = _rms_scale(hv) * gh_ref[:, hs]
            og = _sigmoid(og_ref[pl.ds(r0, L), hs].astype(F32))
            o_ref[pl.ds(r0, L), hs] = (og * hn).astype(BF16)
            wv = (wg * va.astype(F32)).astype(BF16)
            decay = w_inter[L - 1:L, :]
            new_states.append(decay * ct + lax.dot_general(k, wv, tn, preferred_element_type=F32))
        return tuple(new_states)

    init = tuple(jnp.zeros((d, 2 * d), F32) for _ in range(MLSTM_HEADS))
    lax.fori_loop(0, S // L, chunk, init)


def _mlstm(z, cols, rows4, g_head):
    B, S, _ = z.shape
    blk = MLSTM_WIDTH
    base = 3 * FOX_WIDTH // blk
    nc = S // MCHUNK
    return pl.pallas_call(
        _mlstm_kernel,
        grid=(B,),
        in_specs=[pl.BlockSpec((None, S, blk), lambda b: (b, 0, base)),
                  pl.BlockSpec((None, S, blk), lambda b: (b, 0, base + 1)),
                  pl.BlockSpec((None, S, blk), lambda b: (b, 0, base + 2)),
                  pl.BlockSpec((None, S, blk), lambda b: (b, 0, base + 3)),
                  pl.BlockSpec((None, S, LANES), lambda b: (b, 0, 0)),
                  pl.BlockSpec((None, MLSTM_HEADS, nc, MCHUNK), lambda b: (b, 2, 0, 0)),
                  _const_spec((1, blk))],
        out_specs=pl.BlockSpec((None, S, blk), lambda b: (b, 0, 0)),
        out_shape=jax.ShapeDtypeStruct((B, S, blk), BF16),
        compiler_params=pltpu.CompilerParams(dimension_semantics=("arbitrary",),
                                             vmem_limit_bytes=VMEM_LIMIT),
        name="mlstm",
    )(z, z, z, z, cols, rows4, g_head)


def _outproj_kernel(fo_ref, mo_ref, x_ref, mod_ref, g_ref, w_ref, o_ref):
    mix = jnp.dot(fo_ref[...], w_ref[:FOX_WIDTH, :], preferred_element_type=F32) \
        + jnp.dot(mo_ref[...], w_ref[FOX_WIDTH:, :], preferred_element_type=F32)
    gate = mod_ref[2:3, :]
    o_ref[...] = x_ref[...] + gate * (_rms_scale(mix) * g_ref[...])


def _outproj(fox_o, mlstm_o, x, mod, g_post, w_out):
    B, S, D = x.shape
    tm = TM_PROJ
    return pl.pallas_call(
        _outproj_kernel,
        grid=(B, S // tm),
        in_specs=[pl.BlockSpec((None, tm, FOX_WIDTH), lambda b, i: (b, i, 0)),
                  pl.BlockSpec((None, tm, MLSTM_WIDTH), lambda b, i: (b, i, 0)),
                  pl.BlockSpec((None, tm, D), lambda b, i: (b, i, 0)),
                  pl.BlockSpec((None, N_MOD, D), lambda b, i: (b, 0, 0)),
                  _const_spec((1, D)),
                  _const_spec(w_out.shape)],
        out_specs=pl.BlockSpec((None, tm, D), lambda b, i: (b, i, 0)),
        out_shape=jax.ShapeDtypeStruct((B, S, D), F32),
        compiler_params=pltpu.CompilerParams(dimension_semantics=("arbitrary", "arbitrary"),
                                             vmem_limit_bytes=VMEM_LIMIT),
        name="outproj",
    )(fox_o, mlstm_o, x, mod, g_post, w_out)


def _ffn_kernel(x_ref, halo_ref, mod_ref, gpre_ref, gpost_ref, wup_ref, cw_ref, cb_ref, wdn_ref,
                o_ref):
    i = pl.program_id(1)
    tm = x_ref.shape[0]
    x = x_ref[...]
    xf = jnp.concatenate([halo_ref[...], x], axis=0)
    shift = mod_ref[3:4, :]
    scale = mod_ref[4:5, :]
    gate = mod_ref[5:6, :]
    hb = ((_rms_scale(xf) * gpre_ref[...]) * (1.0 + scale) + shift).astype(BF16)
    first = (i > 0).astype(F32)
    row = lax.broadcasted_iota(jnp.int32, (HALO + tm, 1), 0)
    hist = jnp.where(row < HALO, first, 1.0)

    def conv(u, c0):
        u = u * hist
        y = cb_ref[:, c0:c0 + FF_CHUNK]
        for k in range(FFN_CONV):
            off = HALO - (FFN_CONV - 1) + k
            y = y + u[off:off + tm, :] * cw_ref[k:k + 1, c0:c0 + FF_CHUNK]
        return y

    acc = jnp.zeros((tm, D_MODEL), F32)
    for c0 in range(0, D_FF, FF_CHUNK):
        ua = conv(jnp.dot(hb, wup_ref[:, c0:c0 + FF_CHUNK], preferred_element_type=F32), c0)
        ub = conv(jnp.dot(hb, wup_ref[:, D_FF + c0:D_FF + c0 + FF_CHUNK],
                          preferred_element_type=F32), D_FF + c0)
        g = (ua * _sigmoid(ua) * ub).astype(BF16)
        acc = acc + jnp.dot(g, wdn_ref[c0:c0 + FF_CHUNK, :], preferred_element_type=F32)
    o_ref[...] = x + gate * (_rms_scale(acc) * gpost_ref[...])


def _ffn(x, mod, g_pre, g_post, w_up, conv_w, conv_b, w_down):
    B, S, D = x.shape
    tm = TM_PROJ
    hb = tm // HALO
    return pl.pallas_call(
        _ffn_kernel,
        grid=(B, S // tm),
        in_specs=[pl.BlockSpec((None, tm, D), lambda b, i: (b, i, 0)),
                  pl.BlockSpec((None, HALO, D), lambda b, i: (b, jnp.maximum(i * hb - 1, 0), 0)),
                  pl.BlockSpec((None, N_MOD, D), lambda b, i: (b, 0, 0)),
                  _const_spec((1, D)),
                  _const_spec((1, D)),
                  _const_spec(w_up.shape),
                  _const_spec(conv_w.shape),
                  _const_spec(conv_b.shape),
                  _const_spec(w_down.shape)],
        out_specs=pl.BlockSpec((None, tm, D), lambda b, i: (b, i, 0)),
        out_shape=jax.ShapeDtypeStruct((B, S, D), F32),
        compiler_params=pltpu.CompilerParams(dimension_semantics=("arbitrary", "arbitrary"),
                                             vmem_limit_bytes=VMEM_LIMIT),
        name="ffn",
    )(x, x, mod, g_pre, g_post, w_up, conv_w, conv_b, w_down)


def _layer(x, c, w_ada, b_ada, g_pre_mix, g_post_mix, w_in, b_gate, mlstm_conv_w, mlstm_conv_b,
           g_mlstm_head, w_out, g_pre_ffn, g_post_ffn, w_up, ffn_conv_w, ffn_conv_b, w_down):
    B, S, D = x.shape
    mod = _ada(c, w_ada, b_ada).reshape(B, N_MOD, D)

    sizes = (FOX_WIDTH, FOX_WIDTH, FOX_WIDTH, FOX_HEADS, MLSTM_WIDTH, MLSTM_WIDTH, MLSTM_WIDTH,
             MLSTM_HEADS, MLSTM_HEADS, MLSTM_WIDTH)
    offs = [0]
    for s in sizes:
        offs.append(offs[-1] + s)
    col = lambda j: w_in[:, offs[j]:offs[j + 1]]
    n_gate = FOX_HEADS + 2 * MLSTM_HEADS
    w_all = jnp.concatenate(
        [col(0) * (LOG2E * FOX_HEAD_DIM ** -0.5), col(1), col(2), col(4), col(5), col(6), col(9),
         col(3), col(7), col(8), jnp.zeros((D, GATE_COLS - n_gate), F32)], axis=1).astype(BF16)
    bias = jnp.concatenate([b_gate, jnp.zeros((GATE_COLS - n_gate,), F32)]).reshape(1, GATE_COLS)

    z, gates = _inproj(x, mod, g_pre_mix.reshape(1, D), w_all, mlstm_conv_w,
                       mlstm_conv_b.reshape(1, -1))
    cols, rows, qb, kb = _gates(gates, bias)
    fox_o = _fox(z, qb, kb)
    mlstm_o = _mlstm(z, cols, rows.reshape(B, 16, S // MCHUNK, MCHUNK),
                     g_mlstm_head.reshape(1, -1))
    x1 = _outproj(fox_o, mlstm_o, x, mod, g_post_mix.reshape(1, D), w_out.astype(BF16))
    return _ffn(x1, mod, g_pre_ffn.reshape(1, D), g_post_ffn.reshape(1, D), w_up.astype(BF16),
                ffn_conv_w, ffn_conv_b.reshape(1, -1), w_down.astype(BF16))


def kernel(x, c, w_ada, b_ada, g_pre_mix, g_post_mix, w_in, b_gate, mlstm_conv_w, mlstm_conv_b,
           g_mlstm_head, w_out, g_pre_ffn, g_post_ffn, w_up, ffn_conv_w, ffn_conv_b, w_down):
    for l in range(w_ada.shape[0]):
        x = _layer(x, c, w_ada[l], b_ada[l], g_pre_mix[l], g_post_mix[l], w_in[l], b_gate[l],
                   mlstm_conv_w[l], mlstm_conv_b[l], g_mlstm_head[l], w_out[l], g_pre_ffn[l],
                   g_post_ffn[l], w_up[l], ffn_conv_w[l], ffn_conv_b[l], w_down[l])
    return x
```
